```python
import math
import jax, jax.numpy as jnp
from jax import lax
import numpy as np

D_MODEL = 1024
BATCH = 16
SEQ = 4096
DEPTH = 4

CHUNK = 64
SSM_WIDTH = D_MODEL // 2
SSM_GROUP = 16
SSM_GROUPS = SSM_WIDTH // SSM_GROUP
SSM_STATE = 64
DT_MIN = 1e-3
DT_MAX = 1e-1
ATT_WIDTH = D_MODEL // 2
ATT_HEAD_DIM = 64
ATT_HEADS = ATT_WIDTH // ATT_HEAD_DIM
LEFT_CHUNKS = 8
BAND = (LEFT_CHUNKS + 1) * CHUNK
MAX_REL = 128
N_REL = 2 * MAX_REL + 1
D_FF = ((8 * D_MODEL // 3 + 127) // 128) * 128
IN_WIDTH = SSM_WIDTH + 3 * ATT_WIDTH + 2 * D_MODEL
RMS_EPS = 1e-6
MASK_VALUE = -1e30

kernel_name = "hybrid_s5_chunkattn_macaron_sandwich"


def rmsnorm(x, g):
    xf = x.astype(jnp.float32)
    y = xf * lax.rsqrt(jnp.mean(xf * xf, axis=-1, keepdims=True) + RMS_EPS)
    return (y * g.astype(jnp.float32)).astype(x.dtype)


def swiglu(h, w_gate, w_up, w_down):
    return (jax.nn.silu(h @ w_gate) * (h @ w_up)) @ w_down


def s5_scan(u, lam_re, lam_im, log_dt, b_re, b_im, c_re, c_im, d_skip):
    dtype = u.dtype
    bsz, seq, _ = u.shape
    f32 = jnp.float32
    uf = u.astype(f32).reshape(bsz, seq, SSM_GROUPS, SSM_GROUP)
    lr, li = lam_re.astype(f32), lam_im.astype(f32)
    dt = jnp.exp(log_dt.astype(f32))[:, None]
    mag = jnp.exp(lr * dt)
    ang = li * dt
    ab_re, ab_im = mag * jnp.cos(ang), mag * jnp.sin(ang)
    nr, ni = ab_re - 1.0, ab_im
    den = lr * lr + li * li
    f_re = (nr * lr + ni * li) / den
    f_im = (ni * lr - nr * li) / den
    br, bi = b_re.astype(f32), b_im.astype(f32)
    bb_re = f_re[..., None] * br - f_im[..., None] * bi
    bb_im = f_re[..., None] * bi + f_im[..., None] * br
    bu_re = jnp.einsum('bsgh,gph->bsgp', uf, bb_re)
    bu_im = jnp.einsum('bsgh,gph->bsgp', uf, bb_im)
    a_re = jnp.broadcast_to(ab_re, (1, seq, SSM_GROUPS, SSM_STATE))
    a_im = jnp.broadcast_to(ab_im, (1, seq, SSM_GROUPS, SSM_STATE))

    def combine(left, right):
        la_re, la_im, lb_re, lb_im = left
        ra_re, ra_im, rb_re, rb_im = right
        return (ra_re * la_re - ra_im * la_im,
                ra_re * la_im + ra_im * la_re,
                ra_re * lb_re - ra_im * lb_im + rb_re,
                ra_re * lb_im + ra_im * lb_re + rb_im)

    _, _, s_re, s_im = lax.associative_scan(combine, (a_re, a_im, bu_re, bu_im), axis=1)
    y = (jnp.einsum('bsgp,ghp->bsgh', s_re, c_re.astype(f32))
         - jnp.einsum('bsgp,ghp->bsgh', s_im, c_im.astype(f32))
         + d_skip.astype(f32) * uf)
    return y.reshape(bsz, seq, SSM_WIDTH).astype(dtype)


def chunk_attention(q, k, v, rel_bias):
    bsz, seq = q.shape[0], q.shape[1]
    n_chunks = seq // CHUNK
    pad = LEFT_CHUNKS * CHUNK
    kp = jnp.pad(k, ((0, 0), (pad, 0), (0, 0), (0, 0)))
    vp = jnp.pad(v, ((0, 0), (pad, 0), (0, 0), (0, 0)))
    qc = q.reshape(bsz, n_chunks, CHUNK, ATT_HEADS, ATT_HEAD_DIM).transpose(1, 0, 2, 3, 4)
    rel = (jnp.arange(CHUNK)[:, None] + pad) - jnp.arange(BAND)[None, :]
    rel_idx = jnp.clip(rel, -MAX_REL, MAX_REL) + MAX_REL
    bias = rel_bias.astype(jnp.float32)[:, rel_idx]
    scale = ATT_HEAD_DIM ** -0.5

    def one_chunk(args):
        c, q_blk = args
        start = c * CHUNK
        k_band = lax.dynamic_slice_in_dim(kp, start, BAND, axis=1)
        v_band = lax.dynamic_slice_in_dim(vp, start, BAND, axis=1)
        s = jnp.einsum('bqhd,bkhd->bhqk', q_blk, k_band).astype(jnp.float32) * scale + bias
        valid = (start - pad + jnp.arange(BAND)) >= 0
        s = jnp.where(valid, s, MASK_VALUE)
        p = jax.nn.softmax(s, axis=-1).astype(v.dtype)
        return jnp.einsum('bhqk,bkhd->bqhd', p, v_band)

    out = lax.map(one_chunk, (jnp.arange(n_chunks), qc))
    return out.transpose(1, 0, 2, 3, 4).reshape(bsz, seq, ATT_WIDTH)


def hybrid_mixer(h, w_in, lam_re, lam_im, log_dt, b_re, b_im, c_re, c_im, d_skip,
                 w_glu_val, w_glu_gate, w_out_ssm, rel_bias, w_out_att, w_o):
    bsz, seq, _ = h.shape
    proj = h @ w_in
    o1 = SSM_WIDTH
    o2 = o1 + ATT_WIDTH
    o3 = o2 + ATT_WIDTH
    o4 = o3 + ATT_WIDTH
    o5 = o4 + D_MODEL
    u, q, k, v, g_a, g_b = jnp.split(proj, [o1, o2, o3, o4, o5], axis=-1)
    y_a = jax.nn.gelu(s5_scan(u, lam_re, lam_im, log_dt, b_re, b_im, c_re, c_im, d_skip))
    y_a = ((y_a @ w_glu_val) * jax.nn.sigmoid(y_a @ w_glu_gate)) @ w_out_ssm
    hs = (bsz, seq, ATT_HEADS, ATT_HEAD_DIM)
    y_b = chunk_attention(q.reshape(hs), k.reshape(hs), v.reshape(hs), rel_bias) @ w_out_att
    merged = jax.nn.sigmoid(g_a) * y_a + jax.nn.sigmoid(g_b) * y_b
    return merged @ w_o


def setup_inputs(seed: int = 0) -> dict:
    key = jax.random.key(seed)
    ks = jax.random.split(key, 20)
    f32 = jnp.float32
    L, G, P, H = DEPTH, SSM_GROUPS, SSM_STATE, SSM_GROUP

    def nrm(k, shape, scale):
        return jax.random.normal(k, shape, f32) * scale

    x = jax.random.normal(ks[0], (BATCH, SEQ, D_MODEL), f32)
    norm_gains = 1.0 + nrm(ks[1], (L, 6, D_MODEL), 0.05)
    ffn_w_gate = nrm(ks[2], (L, 2, D_MODEL, D_FF), D_MODEL ** -0.5)
    ffn_w_up = nrm(ks[3], (L, 2, D_MODEL, D_FF), D_MODEL ** -0.5)
    ffn_w_down = nrm(ks[4], (L, 2, D_FF, D_MODEL), D_FF ** -0.5)
    w_in = nrm(ks[5], (L, D_MODEL, IN_WIDTH), D_MODEL ** -0.5)
    lam_re = -0.5 + nrm(ks[6], (L, G, P), 0.01)
    lam_im = jnp.pi * jnp.arange(P, dtype=f32) + nrm(ks[7], (L, G, P), 0.01)
    log_dt = math.log(DT_MIN) + jax.random.uniform(ks[8], (L, G), f32) * (math.log(DT_MAX) - math.log(DT_MIN))
    b_re = nrm(ks[9], (L, G, P, H), (2 * H) ** -0.5)
    b_im = nrm(ks[10], (L, G, P, H), (2 * H) ** -0.5)
    c_re = nrm(ks[11], (L, G, H, P), P ** -0.5)
    c_im = nrm(ks[12], (L, G, H, P), P ** -0.5)
    d_skip = nrm(ks[13], (L, G, H), 1.0)
    w_glu_val = nrm(ks[14], (L, SSM_WIDTH, SSM_WIDTH), SSM_WIDTH ** -0.5)
    w_glu_gate = nrm(ks[15], (L, SSM_WIDTH, SSM_WIDTH), SSM_WIDTH ** -0.5)
    w_out_ssm = nrm(ks[16], (L, SSM_WIDTH, D_MODEL), SSM_WIDTH ** -0.5)
    rel_bias = nrm(ks[17], (L, ATT_HEADS, N_REL), 0.1)
    w_out_att = nrm(ks[18], (L, ATT_WIDTH, D_MODEL), ATT_WIDTH ** -0.5)
    w_o = nrm(ks[19], (L, D_MODEL, D_MODEL), D_MODEL ** -0.5)
    return {"x": x, "norm_gains": norm_gains, "ffn_w_gate": ffn_w_gate,
            "ffn_w_up": ffn_w_up, "ffn_w_down": ffn_w_down, "w_in": w_in,
            "lam_re": lam_re, "lam_im": lam_im, "log_dt": log_dt,
            "b_re": b_re, "b_im": b_im, "c_re": c_re, "c_im": c_im,
            "d_skip": d_skip, "w_glu_val": w_glu_val, "w_glu_gate": w_glu_gate,
            "w_out_ssm": w_out_ssm, "rel_bias": rel_bias, "w_out_att": w_out_att,
            "w_o": w_o}


def reference(x, norm_gains, ffn_w_gate, ffn_w_up, ffn_w_down, w_in,
              lam_re, lam_im, log_dt, b_re, b_im, c_re, c_im, d_skip,
              w_glu_val, w_glu_gate, w_out_ssm, rel_bias, w_out_att, w_o):
    for l in range(DEPTH):
        g = norm_gains[l]
        f1 = swiglu(rmsnorm(x, g[0]), ffn_w_gate[l, 0], ffn_w_up[l, 0], ffn_w_down[l, 0])
        x = x + 0.5 * rmsnorm(f1, g[1])
        m = hybrid_mixer(rmsnorm(x, g[2]), w_in[l], lam_re[l], lam_im[l], log_dt[l],
                         b_re[l], b_im[l], c_re[l], c_im[l], d_skip[l],
                         w_glu_val[l], w_glu_gate[l], w_out_ssm[l],
                         rel_bias[l], w_out_att[l], w_o[l])
        x = x + rmsnorm(m, g[3])
        f2 = swiglu(rmsnorm(x, g[4]), ffn_w_gate[l, 1], ffn_w_up[l, 1], ffn_w_down[l, 1])
        x = x + 0.5 * rmsnorm(f2, g[5])
    return x
```

```python
import functools
import math

import jax
import jax.numpy as jnp
import numpy as np
from jax import lax
from jax.experimental import pallas as pl
from jax.experimental.pallas import tpu as pltpu

F32 = jnp.float32
BF16 = jnp.bfloat16

D_MODEL = 1024
D_FF = 2816
CHUNK = 64
SSM_WIDTH = 512
SSM_GROUP = 16
SSM_GROUPS = 32
SSM_STATE = 64
ATT_WIDTH = 512
ATT_HEAD_DIM = 64
ATT_HEADS = 8
LEFT_CHUNKS = 8
MAX_REL = 128
IN_WIDTH = 4096
RMS_EPS = 1e-6
MASK_VALUE = -1e30

S5_L = 16
S5_BLK = S5_L * SSM_GROUP
ATT_TQ = 256
ATT_BAND = ATT_TQ + LEFT_CHUNKS * CHUNK
ATT_PAD = LEFT_CHUNKS * CHUNK

ROW_TILE = 512
FF_CHUNK = 256
VMEM_LIMIT = 56 * 1024 * 1024


def _const_spec(shape):
    nd = len(shape)
    return pl.BlockSpec(shape, lambda *_: (0,) * nd, pipeline_mode=pl.Buffered(1))


def _params(n_axes):
    return pltpu.CompilerParams(dimension_semantics=("arbitrary",) * n_axes,
                                vmem_limit_bytes=VMEM_LIMIT)


def _rmsnorm(x, g):
    ms = jnp.mean(x * x, axis=-1, keepdims=True)
    return x * lax.rsqrt(ms + RMS_EPS) * g


def _ffn_kernel(x_ref, gpre_ref, gpost_ref, wg_ref, wu_ref, wd_ref, o_ref, a_ref):
    x = x_ref[...]
    h = _rmsnorm(x, gpre_ref[...]).astype(BF16)
    for c in range(D_FF // FF_CHUNK):
        sl = slice(c * FF_CHUNK, (c + 1) * FF_CHUNK)
        g = jnp.dot(h, wg_ref[:, sl], preferred_element_type=F32)
        u = jnp.dot(h, wu_ref[:, sl], preferred_element_type=F32)
        a_ref[:, sl] = (g * jax.nn.sigmoid(g) * u).astype(BF16)
    f = jnp.dot(a_ref[...], wd_ref[...], preferred_element_type=F32)
    o_ref[...] = x + 0.5 * _rmsnorm(f, gpost_ref[...])


def _ffn(x, gpre, gpost, wg, wu, wd):
    m = x.shape[0]
    row = pl.BlockSpec((ROW_TILE, D_MODEL), lambda i: (i, 0))
    return pl.pallas_call(
        _ffn_kernel,
        grid=(m // ROW_TILE,),
        in_specs=[row, _const_spec((1, D_MODEL)), _const_spec((1, D_MODEL)),
                  _const_spec((D_MODEL, D_FF)), _const_spec((D_MODEL, D_FF)),
                  _const_spec((D_FF, D_MODEL))],
        out_specs=row,
        out_shape=jax.ShapeDtypeStruct((m, D_MODEL), F32),
        scratch_shapes=[pltpu.VMEM((ROW_TILE, D_FF), BF16)],
        compiler_params=_params(1),
        name="ffn",
    )(x, gpre, gpost, wg, wu, wd)


PROJ_CHUNK = 512


def _proj_kernel(x_ref, g_ref, w_ref, o_ref):
    h = _rmsnorm(x_ref[...], g_ref[...]).astype(BF16)
    for c in range(IN_WIDTH // PROJ_CHUNK):
        sl = slice(c * PROJ_CHUNK, (c + 1) * PROJ_CHUNK)
        r = jnp.dot(h, w_ref[:, sl], preferred_element_type=F32)
        if c == 1:
            r = r * (ATT_HEAD_DIM ** -0.5)
        o_ref[:, sl] = r.astype(BF16)


def _proj(x, g, w_in):
    m = x.shape[0]
    return pl.pallas_call(
        _proj_kernel,
        grid=(m // ROW_TILE,),
        in_specs=[pl.BlockSpec((ROW_TILE, D_MODEL), lambda i: (i, 0)),
                  _const_spec((1, D_MODEL)), _const_spec((D_MODEL, IN_WIDTH))],
        out_specs=pl.BlockSpec((ROW_TILE, IN_WIDTH), lambda i: (i, 0)),
        out_shape=jax.ShapeDtypeStruct((m, IN_WIDTH), BF16),
        compiler_params=_params(1),
        name="in_proj",
    )(x, g, w_in)


def _s5_tables(lam_re, lam_im, log_dt, b_re, b_im, c_re, c_im, d_skip):
    hi = lax.Precision.HIGHEST
    g_, p_, h_ = SSM_GROUPS, SSM_STATE, SSM_GROUP
    lr, li = lam_re.astype(F32), lam_im.astype(F32)
    dt = jnp.exp(log_dt.astype(F32))[:, None]
    mag = jnp.exp(lr * dt)
    ang = li * dt
    ab_re, ab_im = mag * jnp.cos(ang), mag * jnp.sin(ang)
    nr, ni = ab_re - 1.0, ab_im
    den = lr * lr + li * li
    f_re = (nr * lr + ni * li) / den
    f_im = (ni * lr - nr * li) / den
    br, bi = b_re.astype(F32), b_im.astype(F32)
    bb_re = f_re[..., None] * br - f_im[..., None] * bi
    bb_im = f_re[..., None] * bi + f_im[..., None] * br
    d = jnp.arange(S5_L + 1, dtype=F32)[:, None, None]
    pw_mag = jnp.exp(lr * dt * d)
    pw_re = pw_mag * jnp.cos(ang * d)
    pw_im = pw_mag * jnp.sin(ang * d)
    cr, ci = c_re.astype(F32), c_im.astype(F32)
    cp_re = cr[None] * pw_re[:, :, None, :] - ci[None] * pw_im[:, :, None, :]
    cp_im = cr[None] * pw_im[:, :, None, :] + ci[None] * pw_re[:, :, None, :]
    kk = (jnp.einsum('dghp,gpk->dghk', cp_re[:S5_L], bb_re, precision=hi)
          - jnp.einsum('dghp,gpk->dghk', cp_im[:S5_L], bb_im, precision=hi))
    kk = kk.at[0].add(d_skip.astype(F32)[:, :, None] * jnp.eye(h_, dtype=F32)[None])
    jj = np.arange(S5_L)[:, None]
    tt = np.arange(S5_L)[None, :]
    dd = tt - jj
    kt = kk[np.clip(dd, 0, S5_L - 1)]
    kt = jnp.where((dd >= 0)[:, :, None, None, None], kt, 0.0)
    toep = kt.transpose(2, 0, 4, 1, 3).reshape(g_, S5_BLK, S5_BLK)
    rev_re = pw_re[S5_L - 1::-1][:S5_L]
    rev_im = pw_im[S5_L - 1::-1][:S5_L]
    w_re = rev_re[..., None] * bb_re[None] - rev_im[..., None] * bb_im[None]
    w_im = rev_re[..., None] * bb_im[None] + rev_im[..., None] * bb_re[None]
    w_re = w_re.transpose(1, 0, 3, 2).reshape(g_, S5_BLK, p_)
    w_im = w_im.transpose(1, 0, 3, 2).reshape(g_, S5_BLK, p_)
    z = jnp.zeros_like(w_re)
    w_re = w_re.reshape(g_ // 2, 2, S5_BLK, p_)
    w_im = w_im.reshape(g_ // 2, 2, S5_BLK, p_)
    z = z.reshape(g_ // 2, 2, S5_BLK, p_)
    top = jnp.concatenate([w_re[:, 0], z[:, 0], w_im[:, 0], z[:, 0]], axis=-1)
    bot = jnp.concatenate([z[:, 1], w_re[:, 1], z[:, 1], w_im[:, 1]], axis=-1)
    wst = jnp.concatenate([top, bot], axis=1)
    v_re = cp_re[1:].transpose(1, 3, 0, 2).reshape(g_, p_, S5_BLK)
    v_im = -cp_im[1:].transpose(1, 3, 0, 2).reshape(g_, p_, S5_BLK)
    zv = jnp.zeros_like(v_re)
    even = jnp.concatenate([v_re, zv, v_im, zv], axis=1)
    odd = jnp.concatenate([zv, v_re, zv, v_im], axis=1)
    is_even = (jnp.arange(g_) % 2 == 0)[:, None, None]
    vst = jnp.where(is_even, even, odd)
    a16_re = pw_re[S5_L].reshape(1, g_ * p_)
    a16_im = pw_im[S5_L].reshape(1, g_ * p_)
    return toep.astype(BF16), wst.astype(BF16), vst.astype(BF16), a16_re, a16_im


def _s5_kernel(u_ref, toep_ref, wst_ref, vst_ref, are_ref, aim_ref, y_ref,
               scre_ref, scim_ref, sire_ref, siim_ref):
    nblk = u_ref.shape[1]
    for j in range(SSM_GROUPS // 2):
        up = u_ref[0, :, j * 2 * S5_BLK:(j + 1) * 2 * S5_BLK]
        sc = jnp.dot(up, wst_ref[j], preferred_element_type=F32)
        scre_ref[:, j * 128:(j + 1) * 128] = sc[:, :128]
        scim_ref[:, j * 128:(j + 1) * 128] = sc[:, 128:]
    a_re = are_ref[...]
    a_im = aim_ref[...]

    def body(c, carry):
        s_re, s_im = carry
        sire_ref[pl.ds(c, 1), :] = s_re
        siim_ref[pl.ds(c, 1), :] = s_im
        w_re = scre_ref[pl.ds(c, 1), :]
        w_im = scim_ref[pl.ds(c, 1), :]
        n_re = a_re * s_re - a_im * s_im + w_re
        n_im = a_re * s_im + a_im * s_re + w_im
        return n_re, n_im

    zero = jnp.zeros((1, SSM_GROUPS * SSM_STATE), F32)
    lax.fori_loop(0, nblk, body, (zero, zero))
    for g in range(SSM_GROUPS):
        j = g // 2
        ug = u_ref[0, :, g * S5_BLK:(g + 1) * S5_BLK]
        sp = jnp.concatenate([sire_ref[:, j * 128:(j + 1) * 128],
                              siim_ref[:, j * 128:(j + 1) * 128]], axis=1).astype(BF16)
        y = (jnp.dot(ug, toep_ref[g], preferred_element_type=F32)
             + jnp.dot(sp, vst_ref[g], preferred_element_type=F32))
        y_ref[0, :, g * S5_BLK:(g + 1) * S5_BLK] = y.astype(BF16)


def _s5(u_blk, toep, wst, vst, a16_re, a16_im):
    b, nblk, w = u_blk.shape
    nstate = SSM_GROUPS * SSM_STATE
    return pl.pallas_call(
        _s5_kernel,
        grid=(b,),
        in_specs=[pl.BlockSpec((1, nblk, w), lambda i: (i, 0, 0)),
                  _const_spec(toep.shape), _const_spec(wst.shape), _const_spec(vst.shape),
                  _const_spec((1, nstate)), _const_spec((1, nstate))],
        out_specs=pl.BlockSpec((1, nblk, w), lambda i: (i, 0, 0)),
        out_shape=jax.ShapeDtypeStruct((b, nblk, w), BF16),
        scratch_shapes=[pltpu.VMEM((nblk, nstate), F32)] * 4,
        compiler_params=_params(1),
        name="s5",
    )(u_blk, toep, wst, vst, a16_re, a16_im)


def _attn_bias_table(rel_bias):
    qi = np.arange(ATT_TQ)[:, None]
    kj = np.arange(ATT_BAND)[None, :]
    rel = qi + ATT_PAD - kj
    idx = np.clip(rel, -MAX_REL, MAX_REL) + MAX_REL
    dc = kj // CHUNK - qi // CHUNK
    ok = (dc >= 0) & (dc <= LEFT_CHUNKS)
    bias = rel_bias.astype(F32)[:, idx]
    return jnp.where(ok[None], bias, MASK_VALUE)


def _attn_kernel(q_ref, k_ref, v_ref, bias_ref, o_ref, kpad_ref, vpad_ref):
    i = pl.program_id(1)

    @pl.when(i == 0)
    def _():
        kpad_ref[0:ATT_PAD, :] = jnp.zeros((ATT_PAD, ATT_WIDTH), BF16)
        vpad_ref[0:ATT_PAD, :] = jnp.zeros((ATT_PAD, ATT_WIDTH), BF16)
        kpad_ref[ATT_PAD:, :] = k_ref[0]
        vpad_ref[ATT_PAD:, :] = v_ref[0]

    start = pl.multiple_of(i * ATT_TQ, ATT_TQ)
    col = lax.broadcasted_iota(jnp.int32, (ATT_TQ, ATT_BAND), 1)
    valid = col >= (ATT_PAD - start)
    lane = lax.broadcasted_iota(jnp.int32, (ATT_TQ, 128), 1)
    lo = lane < ATT_HEAD_DIM
    for hp in range(ATT_HEADS // 2):
        ls = slice(hp * 128, (hp + 1) * 128)
        q2 = q_ref[0, :, ls]
        k2 = kpad_ref[pl.ds(start, ATT_BAND), ls]
        v2 = vpad_ref[pl.ds(start, ATT_BAND), ls]
        outs = []
        for hh in range(2):
            qm = jnp.where(lo if hh == 0 else jnp.logical_not(lo), q2, jnp.zeros_like(q2))
            s = lax.dot_general(qm, k2, (((1,), (1,)), ((), ())), preferred_element_type=F32)
            s = s + bias_ref[2 * hp + hh]
            s = jnp.where(valid, s, MASK_VALUE)
            m = jnp.max(s, axis=-1, keepdims=True)
            p = jnp.exp(s - m)
            l = jnp.sum(p, axis=-1, keepdims=True)
            o = jnp.dot(p.astype(BF16), v2, preferred_element_type=F32)
            outs.append(o / l)
        o_ref[0, :, ls] = jnp.where(lo, outs[0], outs[1]).astype(BF16)


def _attn(proj3, bias):
    b, s, _ = proj3.shape
    return pl.pallas_call(
        _attn_kernel,
        grid=(b, s // ATT_TQ),
        in_specs=[pl.BlockSpec((1, ATT_TQ, ATT_WIDTH), lambda bi, i: (bi, i, 1)),
                  pl.BlockSpec((1, s, ATT_WIDTH), lambda bi, i: (bi, 0, 2)),
                  pl.BlockSpec((1, s, ATT_WIDTH), lambda bi, i: (bi, 0, 3)),
                  _const_spec(bias.shape)],
        out_specs=pl.BlockSpec((1, ATT_TQ, ATT_WIDTH), lambda bi, i: (bi, i, 0)),
        out_shape=jax.ShapeDtypeStruct((b, s, ATT_WIDTH), BF16),
        scratch_shapes=[pltpu.VMEM((s + ATT_PAD, ATT_WIDTH), BF16)] * 2,
        compiler_params=_params(2),
        name="attn",
    )(proj3, proj3, proj3, bias)


def _mix_kernel(x_ref, ys_ref, at_ref, ga_ref, gb_ref, g_ref,
                wv_ref, wgt_ref, wos_ref, woa_ref, wo_ref, o_ref):
    ya = jax.nn.gelu(ys_ref[...].astype(F32), approximate=True).astype(BF16)
    val = jnp.dot(ya, wv_ref[...], preferred_element_type=F32)
    gate = jnp.dot(ya, wgt_ref[...], preferred_element_type=F32)
    z = (val * jax.nn.sigmoid(gate)).astype(BF16)
    y_a = jnp.dot(z, wos_ref[...], preferred_element_type=F32)
    y_b = jnp.dot(at_ref[...], woa_ref[...], preferred_element_type=F32)
    merged = (jax.nn.sigmoid(ga_ref[...].astype(F32)) * y_a
              + jax.nn.sigmoid(gb_ref[...].astype(F32)) * y_b)
    m = jnp.dot(merged.astype(BF16), wo_ref[...], preferred_element_type=F32)
    o_ref[...] = x_ref[...] + _rmsnorm(m, g_ref[...])


def _mix(x, ys, at, proj, g, wv, wgt, wos, woa, wo):
    m = x.shape[0]
    row = lambda w: pl.BlockSpec((ROW_TILE, w), lambda i: (i, 0))
    return pl.pallas_call(
        _mix_kernel,
        grid=(m // ROW_TILE,),
        in_specs=[row(D_MODEL), row(SSM_WIDTH), row(ATT_WIDTH),
                  pl.BlockSpec((ROW_TILE, D_MODEL), lambda i: (i, 2)),
                  pl.BlockSpec((ROW_TILE, D_MODEL), lambda i: (i, 3)),
                  _const_spec((1, D_MODEL)),
                  _const_spec(wv.shape), _const_spec(wgt.shape), _const_spec(wos.shape),
                  _const_spec(woa.shape), _const_spec(wo.shape)],
        out_specs=row(D_MODEL),
        out_shape=jax.ShapeDtypeStruct((m, D_MODEL), F32),
        compiler_params=_params(1),
        name="mix_out",
    )(x, ys, at, proj, proj, g, wv, wgt, wos, woa, wo)


def kernel(x, norm_gains, ffn_w_gate, ffn_w_up, ffn_w_down, w_in, lam_re, lam_im, log_dt,
           b_re, b_im, c_re, c_im, d_skip, w_glu_val, w_glu_gate, w_out_ssm, rel_bias,
           w_out_att, w_o):
    bsz, seq, _ = x.shape
    m = bsz * seq
    nblk = seq // S5_L
    depth = norm_gains.shape[0]
    xf = x.reshape(m, D_MODEL)
    for l in range(depth):
        g = norm_gains[l].astype(F32).reshape(6, 1, D_MODEL)
        xf = _ffn(xf, g[0], g[1], ffn_w_gate[l, 0].astype(BF16), ffn_w_up[l, 0].astype(BF16),
                  ffn_w_down[l, 0].astype(BF16))
        proj = _proj(xf, g[2], w_in[l].astype(BF16))
        u_blk = proj[:, :SSM_WIDTH].reshape(bsz, nblk, S5_L, SSM_GROUPS, SSM_GROUP)
        u_blk = u_blk.transpose(0, 1, 3, 2, 4).reshape(bsz, nblk, SSM_GROUPS * S5_BLK)
        tabs = _s5_tables(lam_re[l], lam_im[l], log_dt[l], b_re[l], b_im[l], c_re[l], c_im[l],
                          d_skip[l])
        y_blk = _s5(u_blk, *tabs)
        ys = y_blk.reshape(bsz, nblk, SSM_GROUPS, S5_L, SSM_GROUP).transpose(0, 1, 3, 2, 4)
        ys = ys.reshape(m, SSM_WIDTH)
        at = _attn(proj.reshape(bsz, seq, IN_WIDTH), _attn_bias_table(rel_bias[l]))
        at = at.reshape(m, ATT_WIDTH)
        xf = _mix(xf, ys, at, proj, g[3], w_glu_val[l].astype(BF16), w_glu_gate[l].astype(BF16),
                  w_out_ssm[l].astype(BF16), w_out_att[l].astype(BF16), w_o[l].astype(BF16))
        xf = _ffn(xf, g[4], g[5], ffn_w_gate[l, 1].astype(BF16), ffn_w_up[l, 1].astype(BF16),
                  ffn_w_down[l, 1].astype(BF16))
    return xf.reshape(bsz, seq, D_MODEL)
```

```python
import jax
import jax.numpy as jnp
import numpy as np
from jax import lax
from jax.experimental import pallas as pl
from jax.experimental.pallas import tpu as pltpu

F32 = jnp.float32
BF16 = jnp.bfloat16

D_MODEL = 1024
D_FF = 2816
CHUNK = 64
SSM_WIDTH = 512
SSM_GROUP = 16
SSM_GROUPS = 32
SSM_STATE = 64
ATT_WIDTH = 512
ATT_HEAD_DIM = 64
ATT_HEADS = 8
LEFT_CHUNKS = 8
MAX_REL = 128
IN_WIDTH = 4096
RMS_EPS = 1e-6
MASK_VALUE = -1e30

LANES = 128
SUBLANES = 8

ROW_TILE = 512
FF_CHUNK = 256
PROJ_CHUNK = 512
VMEM_LIMIT = 56 * 1024 * 1024

S5_SEQS = SUBLANES
S5_TS = 64
S5_PITCH = S5_TS + SUBLANES
S5_GPT = LANES // SSM_GROUP
S5_TILES = SSM_WIDTH // LANES
S5_NST = SSM_GROUPS * SSM_STATE
S5_TST = S5_GPT * SSM_STATE

ATT_TQ = 256
ATT_PAD = LEFT_CHUNKS * CHUNK
ATT_BAND = ATT_TQ + ATT_PAD
ATT_TAB = 1024


def _const_spec(shape):
    nd = len(shape)
    return pl.BlockSpec(shape, lambda *_: (0,) * nd, pipeline_mode=pl.Buffered(1))


def _layer_spec(shape, *lead):
    nd = len(shape)
    return pl.BlockSpec((None,) * len(lead) + shape, lambda *_: lead + (0,) * nd,
                        pipeline_mode=pl.Buffered(1))


def _params(n_axes):
    return pltpu.CompilerParams(dimension_semantics=("arbitrary",) * n_axes,
                                vmem_limit_bytes=VMEM_LIMIT)


def _rmsnorm(x, g):
    ms = jnp.mean(x * x, axis=-1, keepdims=True)
    return x * lax.rsqrt(ms + RMS_EPS) * g


def _ffn_kernel(x_ref, gpre_ref, gpost_ref, wg_ref, wu_ref, wd_ref, o_ref, a_ref):
    x = x_ref[...]
    h = _rmsnorm(x, gpre_ref[...]).astype(BF16)
    for c in range(D_FF // FF_CHUNK):
        sl = slice(c * FF_CHUNK, (c + 1) * FF_CHUNK)
        g = jnp.dot(h, wg_ref[:, sl], preferred_element_type=F32)
        u = jnp.dot(h, wu_ref[:, sl], preferred_element_type=F32)
        a_ref[:, sl] = (g * jax.nn.sigmoid(g) * u).astype(BF16)
    f = jnp.dot(a_ref[...], wd_ref[...], preferred_element_type=F32)
    o_ref[...] = x + 0.5 * _rmsnorm(f, gpost_ref[...])


def _ffn(x, gains, l, j, wg, wu, wd):
    m = x.shape[0]
    row = pl.BlockSpec((ROW_TILE, D_MODEL), lambda i: (i, 0))
    return pl.pallas_call(
        _ffn_kernel,
        grid=(m // ROW_TILE,),
        in_specs=[row, _layer_spec((1, D_MODEL), l, 4 * j), _layer_spec((1, D_MODEL), l, 4 * j + 1),
                  _layer_spec((D_MODEL, D_FF), l, j), _layer_spec((D_MODEL, D_FF), l, j),
                  _layer_spec((D_FF, D_MODEL), l, j)],
        out_specs=row,
        out_shape=jax.ShapeDtypeStruct((m, D_MODEL), F32),
        scratch_shapes=[pltpu.VMEM((ROW_TILE, D_FF), BF16)],
        compiler_params=_params(1),
        name="ffn",
    )(x, gains, gains, wg, wu, wd)


def _proj_kernel(x_ref, g_ref, w_ref, o_ref):
    h = _rmsnorm(x_ref[...], g_ref[...]).astype(BF16)
    for c in range(IN_WIDTH // PROJ_CHUNK):
        sl = slice(c * PROJ_CHUNK, (c + 1) * PROJ_CHUNK)
        r = jnp.dot(h, w_ref[:, sl], preferred_element_type=F32)
        if c == 1:
            r = r * (ATT_HEAD_DIM ** -0.5)
        o_ref[:, sl] = r.astype(BF16)


def _proj(x, gains, l, w_in):
    m = x.shape[0]
    return pl.pallas_call(
        _proj_kernel,
        grid=(m // ROW_TILE,),
        in_specs=[pl.BlockSpec((ROW_TILE, D_MODEL), lambda i: (i, 0)),
                  _layer_spec((1, D_MODEL), l, 2), _layer_spec((D_MODEL, IN_WIDTH), l)],
        out_specs=pl.BlockSpec((ROW_TILE, IN_WIDTH), lambda i: (i, 0)),
        out_shape=jax.ShapeDtypeStruct((m, IN_WIDTH), BF16),
        compiler_params=_params(1),
        name="in_proj",
    )(x, gains, w_in)


def _s5_tables(lam_re, lam_im, log_dt, b_re, b_im, c_re, c_im, d_skip):
    lr, li = lam_re.astype(F32), lam_im.astype(F32)
    dt = jnp.exp(log_dt.astype(F32))[:, None]
    mag = jnp.exp(lr * dt)
    ang = li * dt
    ab_re, ab_im = mag * jnp.cos(ang), mag * jnp.sin(ang)
    nr, ni = ab_re - 1.0, ab_im
    den = lr * lr + li * li
    f_re = (nr * lr + ni * li) / den
    f_im = (ni * lr - nr * li) / den
    br, bi = b_re.astype(F32), b_im.astype(F32)
    bb_re = f_re[..., None] * br - f_im[..., None] * bi
    bb_im = f_re[..., None] * bi + f_im[..., None] * br
    eye = jnp.eye(S5_GPT, dtype=F32)

    def expand_b(bb):
        t = bb.reshape(S5_TILES, S5_GPT, SSM_STATE, SSM_GROUP)
        t = t.transpose(0, 1, 3, 2)[:, :, :, None, :] * eye[None, :, None, :, None]
        return t.reshape(S5_TILES, LANES, S5_TST)

    def expand_c(cc):
        t = cc.reshape(S5_TILES, S5_GPT, SSM_GROUP, SSM_STATE)
        t = t.transpose(0, 1, 3, 2)[:, :, :, None, :] * eye[None, :, None, :, None]
        return t.reshape(S5_TILES, S5_TST, LANES)

    bq = jnp.concatenate([expand_b(bb_re), expand_b(bb_im)], axis=2).astype(BF16)
    cq = jnp.concatenate([expand_c(c_re.astype(F32)), expand_c(-c_im.astype(F32))], axis=1).astype(BF16)
    n_lt = S5_NST // LANES
    rep = lambda a: jnp.broadcast_to(a.reshape(n_lt, 1, LANES), (n_lt, S5_SEQS, LANES))
    return (bq, cq, rep(ab_re), rep(ab_im),
            d_skip.astype(F32).reshape(1, SSM_WIDTH))


def _s5_kernel(u_ref, bq_ref, cq_ref, are_ref, aim_ref, d_ref, y_ref,
               sre_ref, sim_ref, cre_ref, cim_ref, ubt_ref, utm_ref):
    rows = S5_SEQS * S5_TS
    n_lt = S5_NST // LANES
    lt_per_q = S5_TST // LANES

    @pl.when(pl.program_id(1) == 0)
    def _():
        cre_ref[...] = jnp.zeros_like(cre_ref)
        cim_ref[...] = jnp.zeros_like(cim_ref)

    for b in range(S5_SEQS):
        ub = u_ref[b].astype(F32)
        for q in range(S5_TILES):
            ubt_ref[q, b * S5_PITCH:b * S5_PITCH + S5_TS, :] = ub[:, q * LANES:(q + 1) * LANES]
    for t in range(S5_TS):
        for q in range(S5_TILES):
            utm_ref[q, t * S5_SEQS:(t + 1) * S5_SEQS, :] = ubt_ref[q, pl.ds(t, S5_SEQS, stride=S5_PITCH), :]
    u = jnp.concatenate([utm_ref[q] for q in range(S5_TILES)], axis=1)
    ub16 = u.astype(BF16)
    for q in range(S5_TILES):
        bu = jnp.dot(ub16[:, q * LANES:(q + 1) * LANES], bq_ref[q], preferred_element_type=F32)
        for k in range(lt_per_q):
            sre_ref[q * lt_per_q + k] = bu[:, k * LANES:(k + 1) * LANES]
            sim_ref[q * lt_per_q + k] = bu[:, S5_TST + k * LANES:S5_TST + (k + 1) * LANES]

    def step(t, carry):
        s_re, s_im = carry
        idx = pl.ds(pl.multiple_of(t * S5_SEQS, S5_SEQS), S5_SEQS)
        n_re, n_im = [], []
        for j in range(n_lt):
            a_re = are_ref[j]
            a_im = aim_ref[j]
            r = a_re * s_re[j] - a_im * s_im[j] + sre_ref[j, idx, :]
            i = a_re * s_im[j] + a_im * s_re[j] + sim_ref[j, idx, :]
            sre_ref[j, idx, :] = r
            sim_ref[j, idx, :] = i
            n_re.append(r)
            n_im.append(i)
        return n_re, n_im

    init = ([cre_ref[j] for j in range(n_lt)], [cim_ref[j] for j in range(n_lt)])
    s_re, s_im = lax.fori_loop(0, S5_TS, step, init, unroll=4)
    for j in range(n_lt):
        cre_ref[j] = s_re[j]
        cim_ref[j] = s_im[j]
    du = d_ref[...] * u
    for q in range(S5_TILES):
        tiles = ([sre_ref[q * lt_per_q + k] for k in range(lt_per_q)]
                 + [sim_ref[q * lt_per_q + k] for k in range(lt_per_q)])
        s = jnp.concatenate(tiles, axis=1).astype(BF16)
        utm_ref[q] = jnp.dot(s, cq_ref[q], preferred_element_type=F32) + du[:, q * LANES:(q + 1) * LANES]
    for b in range(S5_SEQS):
        for q in range(S5_TILES):
            yb = utm_ref[q, pl.ds(b, S5_TS, stride=S5_SEQS), :]
            y_ref[b, :, q * LANES:(q + 1) * LANES] = yb.astype(BF16)


def _s5(proj3, bq, cq, a_re, a_im, d):
    b, s, _ = proj3.shape
    rows = S5_SEQS * S5_TS
    n_lt = S5_NST // LANES
    blk = pl.BlockSpec((S5_SEQS, S5_TS, SSM_WIDTH), lambda g, i: (g, i, 0))
    return pl.pallas_call(
        _s5_kernel,
        grid=(b // S5_SEQS, s // S5_TS),
        in_specs=[blk, _const_spec(bq.shape), _const_spec(cq.shape), _const_spec(a_re.shape),
                  _const_spec(a_im.shape), _const_spec(d.shape)],
        out_specs=blk,
        out_shape=jax.ShapeDtypeStruct((b, s, SSM_WIDTH), BF16),
        scratch_shapes=[pltpu.VMEM((n_lt, rows, LANES), F32), pltpu.VMEM((n_lt, rows, LANES), F32),
                        pltpu.VMEM((n_lt, S5_SEQS, LANES), F32), pltpu.VMEM((n_lt, S5_SEQS, LANES), F32),
                        pltpu.VMEM((S5_TILES, S5_SEQS * S5_PITCH, LANES), F32),
                        pltpu.VMEM((S5_TILES, rows, LANES), F32)],
        compiler_params=_params(2),
        name="s5",
    )(proj3, bq, cq, a_re, a_im, d)


def _attn_bias_line(rel_bias):
    rb = rel_bias.astype(F32)
    n_far = ATT_BAND - 1 - MAX_REL
    n_neg = ATT_TAB - n_far - (2 * MAX_REL + 1)
    return jnp.concatenate([jnp.broadcast_to(rb[:, -1:], (ATT_HEADS, n_far)), rb[:, ::-1],
                            jnp.broadcast_to(rb[:, :1], (ATT_HEADS, n_neg))], axis=1)


def _attn_kernel(q_ref, k_ref, v_ref, line_ref, o_ref, kpad_ref, vpad_ref, bias_ref):
    i = pl.program_id(1)

    @pl.when(jnp.logical_and(pl.program_id(0) == 0, i == 0))
    def _():
        qi = lax.broadcasted_iota(jnp.int32, (ATT_TQ, ATT_BAND), 0)
        kj = lax.broadcasted_iota(jnp.int32, (ATT_TQ, ATT_BAND), 1)
        dc = kj // CHUNK - qi // CHUNK
        in_band = jnp.logical_and(dc >= 0, dc <= LEFT_CHUNKS)
        for h in range(ATT_HEADS):
            line = jnp.broadcast_to(line_ref[h:h + 1, :], (ATT_TQ, ATT_TAB))
            skew = pltpu.roll(line, ATT_TAB - (ATT_TQ - 1), 1, stride=1, stride_axis=0)
            bias_ref[h] = jnp.where(in_band, skew[:, :ATT_BAND], MASK_VALUE)

    @pl.when(i == 0)
    def _():
        kpad_ref[0:ATT_PAD, :] = jnp.zeros((ATT_PAD, ATT_WIDTH), BF16)
        vpad_ref[0:ATT_PAD, :] = jnp.zeros((ATT_PAD, ATT_WIDTH), BF16)
        kpad_ref[ATT_PAD:, :] = k_ref[0]
        vpad_ref[ATT_PAD:, :] = v_ref[0]

    start = pl.multiple_of(i * ATT_TQ, ATT_TQ)
    col = lax.broadcasted_iota(jnp.int32, (ATT_TQ, ATT_BAND), 1)
    valid = col >= (ATT_PAD - start)
    lane = lax.broadcasted_iota(jnp.int32, (ATT_TQ, LANES), 1)
    lo = lane < ATT_HEAD_DIM
    for hp in range(ATT_HEADS // 2):
        ls = slice(hp * LANES, (hp + 1) * LANES)
        q2 = q_ref[0, :, ls]
        k2 = kpad_ref[pl.ds(start, ATT_BAND), ls]
        v2 = vpad_ref[pl.ds(start, ATT_BAND), ls]
        outs = []
        for hh in range(2):
            qm = jnp.where(lo if hh == 0 else jnp.logical_not(lo), q2, jnp.zeros_like(q2))
            s = lax.dot_general(qm, k2, (((1,), (1,)), ((), ())), preferred_element_type=F32)
            s = s + bias_ref[2 * hp + hh]
            s = jnp.where(valid, s, MASK_VALUE)
            m = jnp.max(s, axis=-1, keepdims=True)
            p = jnp.exp(s - m)
            l = jnp.sum(p, axis=-1, keepdims=True)
            o = jnp.dot(p.astype(BF16), v2, preferred_element_type=F32)
            outs.append(o / l)
        o_ref[0, :, ls] = jnp.where(lo, outs[0], outs[1]).astype(BF16)


def _attn(proj3, line):
    b, s, _ = proj3.shape
    return pl.pallas_call(
        _attn_kernel,
        grid=(b, s // ATT_TQ),
        in_specs=[pl.BlockSpec((1, ATT_TQ, ATT_WIDTH), lambda bi, i: (bi, i, 1)),
                  pl.BlockSpec((1, s, ATT_WIDTH), lambda bi, i: (bi, 0, 2)),
                  pl.BlockSpec((1, s, ATT_WIDTH), lambda bi, i: (bi, 0, 3)),
                  _const_spec(line.shape)],
        out_specs=pl.BlockSpec((1, ATT_TQ, ATT_WIDTH), lambda bi, i: (bi, i, 0)),
        out_shape=jax.ShapeDtypeStruct((b, s, ATT_WIDTH), BF16),
        scratch_shapes=[pltpu.VMEM((s + ATT_PAD, ATT_WIDTH), BF16),
                        pltpu.VMEM((s + ATT_PAD, ATT_WIDTH), BF16),
                        pltpu.VMEM((ATT_HEADS, ATT_TQ, ATT_BAND), F32)],
        compiler_params=_params(2),
        name="attn",
    )(proj3, proj3, proj3, line)


def _mix_kernel(x_ref, ys_ref, at_ref, ga_ref, gb_ref, g_ref,
                wv_ref, wgt_ref, wos_ref, woa_ref, wo_ref, o_ref):
    ya = jax.nn.gelu(ys_ref[...].astype(F32), approximate=True).astype(BF16)
    val = jnp.dot(ya, wv_ref[...], preferred_element_type=F32)
    gate = jnp.dot(ya, wgt_ref[...], preferred_element_type=F32)
    z = (val * jax.nn.sigmoid(gate)).astype(BF16)
    y_a = jnp.dot(z, wos_ref[...], preferred_element_type=F32)
    y_b = jnp.dot(at_ref[...], woa_ref[...], preferred_element_type=F32)
    merged = (jax.nn.sigmoid(ga_ref[...].astype(F32)) * y_a
              + jax.nn.sigmoid(gb_ref[...].astype(F32)) * y_b)
    m = jnp.dot(merged.astype(BF16), wo_ref[...], preferred_element_type=F32)
    o_ref[...] = x_ref[...] + _rmsnorm(m, g_ref[...])


def _mix(x, ys, at, proj, gains, l, wv, wgt, wos, woa, wo):
    m = x.shape[0]
    row = lambda w: pl.BlockSpec((ROW_TILE, w), lambda i: (i, 0))
    return pl.pallas_call(
        _mix_kernel,
        grid=(m // ROW_TILE,),
        in_specs=[row(D_MODEL), row(SSM_WIDTH), row(ATT_WIDTH),
                  pl.BlockSpec((ROW_TILE, D_MODEL), lambda i: (i, 2)),
                  pl.BlockSpec((ROW_TILE, D_MODEL), lambda i: (i, 3)),
                  _layer_spec((1, D_MODEL), l, 3),
                  _layer_spec((SSM_WIDTH, SSM_WIDTH), l), _layer_spec((SSM_WIDTH, SSM_WIDTH), l),
                  _layer_spec((SSM_WIDTH, D_MODEL), l), _layer_spec((ATT_WIDTH, D_MODEL), l),
                  _layer_spec((D_MODEL, D_MODEL), l)],
        out_specs=row(D_MODEL),
        out_shape=jax.ShapeDtypeStruct((m, D_MODEL), F32),
        compiler_params=_params(1),
        name="mix_out",
    )(x, ys, at, proj, proj, gains, wv, wgt, wos, woa, wo)


def kernel(x, norm_gains, ffn_w_gate, ffn_w_up, ffn_w_down, w_in, lam_re, lam_im, log_dt,
           b_re, b_im, c_re, c_im, d_skip, w_glu_val, w_glu_gate, w_out_ssm, rel_bias,
           w_out_att, w_o):
    bsz, seq, _ = x.shape
    m = bsz * seq
    depth = norm_gains.shape[0]
    gains = norm_gains.astype(F32).reshape(depth, 6, 1, D_MODEL)
    wg, wu, wd = ffn_w_gate.astype(BF16), ffn_w_up.astype(BF16), ffn_w_down.astype(BF16)
    w_in, w_o = w_in.astype(BF16), w_o.astype(BF16)
    wv, wgt = w_glu_val.astype(BF16), w_glu_gate.astype(BF16)
    wos, woa = w_out_ssm.astype(BF16), w_out_att.astype(BF16)
    xf = x.reshape(m, D_MODEL)
    for l in range(depth):
        xf = _ffn(xf, gains, l, 0, wg, wu, wd)
        proj = _proj(xf, gains, l, w_in)
        proj3 = proj.reshape(bsz, seq, IN_WIDTH)
        tabs = _s5_tables(lam_re[l], lam_im[l], log_dt[l], b_re[l], b_im[l], c_re[l], c_im[l],
                          d_skip[l])
        ys = _s5(proj3, *tabs).reshape(m, SSM_WIDTH)
        at = _attn(proj3, _attn_bias_line(rel_bias[l])).reshape(m, ATT_WIDTH)
        xf = _mix(xf, ys, at, proj, gains, l, wv, wgt, wos, woa, w_o)
        xf = _ffn(xf, gains, l, 1, wg, wu, wd)
    return xf.reshape(bsz, seq, D_MODEL)
```

```python
import functools

import jax
import jax.numpy as jnp
import numpy as np
from jax import lax
from jax.experimental import pallas as pl
from jax.experimental.pallas import tpu as pltpu

F32 = jnp.float32
BF16 = jnp.bfloat16

D_MODEL = 1024
D_FF = 2816
CHUNK = 64
SSM_WIDTH = 512
SSM_GROUP = 16
SSM_GROUPS = 32
SSM_STATE = 64
ATT_WIDTH = 512
ATT_HEAD_DIM = 64
ATT_HEADS = 8
LEFT_CHUNKS = 8
MAX_REL = 128
IN_WIDTH = 4096
RMS_EPS = 1e-6
MASK_VALUE = -1e30

LANES = 128
SUBLANES = 8

ROW_TILE = 512
FF_CHUNK = 256
PROJ_CHUNK = 512
VMEM_LIMIT = 56 * 1024 * 1024

S5_SEQS = SUBLANES
S5_TS = 64
S5_PITCH = S5_TS + SUBLANES
S5_NBUF = 3
S5_GPT = LANES // SSM_GROUP
S5_TILES = SSM_WIDTH // LANES
S5_NST = SSM_GROUPS * SSM_STATE
S5_TST = S5_GPT * SSM_STATE

ATT_TQ = 256
ATT_PAD = LEFT_CHUNKS * CHUNK
ATT_BAND = ATT_TQ + ATT_PAD
ATT_TAB = 1024


def _const_spec(shape):
    nd = len(shape)
    return pl.BlockSpec(shape, lambda *_: (0,) * nd, pipeline_mode=pl.Buffered(1))


def _layer_spec(shape, *lead):
    nd = len(shape)
    return pl.BlockSpec((None,) * len(lead) + shape, lambda *_: lead + (0,) * nd,
                        pipeline_mode=pl.Buffered(1))


def _params(n_axes):
    return pltpu.CompilerParams(dimension_semantics=("arbitrary",) * n_axes,
                                vmem_limit_bytes=VMEM_LIMIT)


def _rmsnorm(x, g):
    ms = jnp.mean(x * x, axis=-1, keepdims=True)
    return x * lax.rsqrt(ms + RMS_EPS) * g


def _ffn_kernel(x_ref, gpre_ref, gpost_ref, wg_ref, wu_ref, wd_ref, o_ref, a_ref):
    x = x_ref[...]
    h = _rmsnorm(x, gpre_ref[...]).astype(BF16)
    for c in range(D_FF // FF_CHUNK):
        sl = slice(c * FF_CHUNK, (c + 1) * FF_CHUNK)
        g = jnp.dot(h, wg_ref[:, sl], preferred_element_type=F32)
        u = jnp.dot(h, wu_ref[:, sl], preferred_element_type=F32)
        a_ref[:, sl] = (g * jax.nn.sigmoid(g) * u).astype(BF16)
    f = jnp.dot(a_ref[...], wd_ref[...], preferred_element_type=F32)
    o_ref[...] = x + 0.5 * _rmsnorm(f, gpost_ref[...])


def _ffn(x, gains, l, j, wg, wu, wd):
    m = x.shape[0]
    row = pl.BlockSpec((ROW_TILE, D_MODEL), lambda i: (i, 0))
    return pl.pallas_call(
        _ffn_kernel,
        grid=(m // ROW_TILE,),
        in_specs=[row, _layer_spec((1, D_MODEL), l, 4 * j), _layer_spec((1, D_MODEL), l, 4 * j + 1),
                  _layer_spec((D_MODEL, D_FF), l, j), _layer_spec((D_MODEL, D_FF), l, j),
                  _layer_spec((D_FF, D_MODEL), l, j)],
        out_specs=row,
        out_shape=jax.ShapeDtypeStruct((m, D_MODEL), F32),
        scratch_shapes=[pltpu.VMEM((ROW_TILE, D_FF), BF16)],
        compiler_params=_params(1),
        name="ffn",
    )(x, gains, gains, wg, wu, wd)


def _proj_kernel(x_ref, g_ref, w_ref, o_ref):
    h = _rmsnorm(x_ref[...], g_ref[...]).astype(BF16)
    for c in range(IN_WIDTH // PROJ_CHUNK):
        sl = slice(c * PROJ_CHUNK, (c + 1) * PROJ_CHUNK)
        r = jnp.dot(h, w_ref[:, sl], preferred_element_type=F32)
        if c == 1:
            r = r * (ATT_HEAD_DIM ** -0.5)
        o_ref[:, sl] = r.astype(BF16)


def _proj(x, gains, l, w_in):
    m = x.shape[0]
    return pl.pallas_call(
        _proj_kernel,
        grid=(m // ROW_TILE,),
        in_specs=[pl.BlockSpec((ROW_TILE, D_MODEL), lambda i: (i, 0)),
                  _layer_spec((1, D_MODEL), l, 2), _layer_spec((D_MODEL, IN_WIDTH), l)],
        out_specs=pl.BlockSpec((ROW_TILE, IN_WIDTH), lambda i: (i, 0)),
        out_shape=jax.ShapeDtypeStruct((m, IN_WIDTH), BF16),
        compiler_params=_params(1),
        name="in_proj",
    )(x, gains, w_in)


def _s5_tables(lam_re, lam_im, log_dt, b_re, b_im, c_re, c_im, d_skip):
    lr, li = lam_re.astype(F32), lam_im.astype(F32)
    dt = jnp.exp(log_dt.astype(F32))[:, None]
    mag = jnp.exp(lr * dt)
    ang = li * dt
    ab_re, ab_im = mag * jnp.cos(ang), mag * jnp.sin(ang)
    nr, ni = ab_re - 1.0, ab_im
    den = lr * lr + li * li
    f_re = (nr * lr + ni * li) / den
    f_im = (ni * lr - nr * li) / den
    br, bi = b_re.astype(F32), b_im.astype(F32)
    bb_re = f_re[..., None] * br - f_im[..., None] * bi
    bb_im = f_re[..., None] * bi + f_im[..., None] * br
    eye = jnp.eye(S5_GPT, dtype=F32)

    def expand_b(bb):
        t = bb.reshape(S5_TILES, S5_GPT, SSM_STATE, SSM_GROUP)
        t = t.transpose(0, 1, 3, 2)[:, :, :, None, :] * eye[None, :, None, :, None]
        return t.reshape(S5_TILES, LANES, S5_TST)

    def expand_c(cc):
        t = cc.reshape(S5_TILES, S5_GPT, SSM_GROUP, SSM_STATE)
        t = t.transpose(0, 1, 3, 2)[:, :, :, None, :] * eye[None, :, None, :, None]
        return t.reshape(S5_TILES, S5_TST, LANES)

    bq = jnp.concatenate([expand_b(bb_re), expand_b(bb_im)], axis=2).astype(BF16)
    cq = jnp.concatenate([expand_c(c_re.astype(F32)), expand_c(-c_im.astype(F32))], axis=1).astype(BF16)
    n_lt = S5_NST // LANES
    rep = lambda a: jnp.broadcast_to(a.reshape(n_lt, 1, LANES), (n_lt, S5_SEQS, LANES))
    return (bq, cq, rep(ab_re), rep(ab_im),
            d_skip.astype(F32).reshape(1, SSM_WIDTH))


def _s5_kernel(u_ref, bq_ref, cq_ref, are_ref, aim_ref, d_ref, y_ref, *scratch):
    sre = scratch[0:S5_NBUF]
    sim = scratch[S5_NBUF:2 * S5_NBUF]
    utm = scratch[2 * S5_NBUF:3 * S5_NBUF]
    cre_ref, cim_ref, ubt_ref, ytm_ref = scratch[3 * S5_NBUF:]
    i = pl.program_id(1)
    n_lt = S5_NST // LANES
    lt_per_q = S5_TST // LANES

    @pl.when(i == 0)
    def _():
        for ref in scratch[:3 * S5_NBUF] + (cre_ref, cim_ref):
            ref[...] = jnp.zeros_like(ref)

    def stages(slot_in):
        slot_scan = (slot_in + S5_NBUF - 1) % S5_NBUF
        slot_out = (slot_in + S5_NBUF - 2) % S5_NBUF

        for b in range(S5_SEQS):
            ub = u_ref[b].astype(F32)
            for q in range(S5_TILES):
                ubt_ref[q, b * S5_PITCH:b * S5_PITCH + S5_TS, :] = ub[:, q * LANES:(q + 1) * LANES]
        for t in range(S5_TS):
            for q in range(S5_TILES):
                utm[slot_in][q, t * S5_SEQS:(t + 1) * S5_SEQS, :] = (
                    ubt_ref[q, pl.ds(t, S5_SEQS, stride=S5_PITCH), :])
        u_in = jnp.concatenate([utm[slot_in][q] for q in range(S5_TILES)], axis=1).astype(BF16)
        for q in range(S5_TILES):
            bu = jnp.dot(u_in[:, q * LANES:(q + 1) * LANES], bq_ref[q], preferred_element_type=F32)
            for k in range(lt_per_q):
                sre[slot_in][q * lt_per_q + k] = bu[:, k * LANES:(k + 1) * LANES]
                sim[slot_in][q * lt_per_q + k] = bu[:, S5_TST + k * LANES:S5_TST + (k + 1) * LANES]

        for j in range(n_lt):
            a_re = are_ref[j]
            a_im = aim_ref[j]
            s_re = cre_ref[j]
            s_im = cim_ref[j]
            for t in range(S5_TS):
                idx = slice(t * S5_SEQS, (t + 1) * S5_SEQS)
                n_re = a_re * s_re - a_im * s_im + sre[slot_scan][j, idx, :]
                n_im = a_re * s_im + a_im * s_re + sim[slot_scan][j, idx, :]
                sre[slot_scan][j, idx, :] = n_re
                sim[slot_scan][j, idx, :] = n_im
                s_re, s_im = n_re, n_im
            cre_ref[j] = s_re
            cim_ref[j] = s_im

        u_out = jnp.concatenate([utm[slot_out][q] for q in range(S5_TILES)], axis=1)
        du = d_ref[...] * u_out
        for q in range(S5_TILES):
            tiles = ([sre[slot_out][q * lt_per_q + k] for k in range(lt_per_q)]
                     + [sim[slot_out][q * lt_per_q + k] for k in range(lt_per_q)])
            s = jnp.concatenate(tiles, axis=1).astype(BF16)
            ytm_ref[q] = jnp.dot(s, cq_ref[q], preferred_element_type=F32) + du[:, q * LANES:(q + 1) * LANES]
        for b in range(S5_SEQS):
            for q in range(S5_TILES):
                yb = ytm_ref[q, pl.ds(b, S5_TS, stride=S5_SEQS), :]
                y_ref[b, :, q * LANES:(q + 1) * LANES] = yb.astype(BF16)

    for r in range(S5_NBUF):
        pl.when(lax.rem(i, S5_NBUF) == r)(functools.partial(stages, r))


def _s5(proj3, bq, cq, a_re, a_im, d):
    b, s, _ = proj3.shape
    rows = S5_SEQS * S5_TS
    n_lt = S5_NST // LANES
    n_t = s // S5_TS
    return pl.pallas_call(
        _s5_kernel,
        grid=(b // S5_SEQS, n_t + S5_NBUF - 1),
        in_specs=[pl.BlockSpec((S5_SEQS, S5_TS, SSM_WIDTH), lambda g, i: (g, jnp.minimum(i, n_t - 1), 0)),
                  _const_spec(bq.shape), _const_spec(cq.shape), _const_spec(a_re.shape),
                  _const_spec(a_im.shape), _const_spec(d.shape)],
        out_specs=pl.BlockSpec((S5_SEQS, S5_TS, SSM_WIDTH),
                               lambda g, i: (g, jnp.maximum(i - (S5_NBUF - 1), 0), 0)),
        out_shape=jax.ShapeDtypeStruct((b, s, SSM_WIDTH), BF16),
        scratch_shapes=([pltpu.VMEM((n_lt, rows, LANES), F32)] * (2 * S5_NBUF)
                        + [pltpu.VMEM((S5_TILES, rows, LANES), F32)] * S5_NBUF
                        + [pltpu.VMEM((n_lt, S5_SEQS, LANES), F32), pltpu.VMEM((n_lt, S5_SEQS, LANES), F32),
                           pltpu.VMEM((S5_TILES, S5_SEQS * S5_PITCH, LANES), F32),
                           pltpu.VMEM((S5_TILES, rows, LANES), F32)]),
        compiler_params=_params(2),
        name="s5",
    )(proj3, bq, cq, a_re, a_im, d)


def _attn_bias_line(rel_bias):
    rb = rel_bias.astype(F32)
    n_far = ATT_BAND - 1 - MAX_REL
    n_neg = ATT_TAB - n_far - (2 * MAX_REL + 1)
    return jnp.concatenate([jnp.broadcast_to(rb[:, -1:], (ATT_HEADS, n_far)), rb[:, ::-1],
                            jnp.broadcast_to(rb[:, :1], (ATT_HEADS, n_neg))], axis=1)


def _attn_kernel(q_ref, k_ref, v_ref, line_ref, o_ref, kpad_ref, vpad_ref, bias_ref):
    i = pl.program_id(1)

    @pl.when(jnp.logical_and(pl.program_id(0) == 0, i == 0))
    def _():
        qi = lax.broadcasted_iota(jnp.int32, (ATT_TQ, ATT_BAND), 0)
        kj = lax.broadcasted_iota(jnp.int32, (ATT_TQ, ATT_BAND), 1)
        dc = kj // CHUNK - qi // CHUNK
        in_band = jnp.logical_and(dc >= 0, dc <= LEFT_CHUNKS)
        for h in range(ATT_HEADS):
            line = jnp.broadcast_to(line_ref[h:h + 1, :], (ATT_TQ, ATT_TAB))
            skew = pltpu.roll(line, ATT_TAB - (ATT_TQ - 1), 1, stride=1, stride_axis=0)
            bias_ref[h] = jnp.where(in_band, skew[:, :ATT_BAND], MASK_VALUE)

    @pl.when(i == 0)
    def _():
        kpad_ref[0:ATT_PAD, :] = jnp.zeros((ATT_PAD, ATT_WIDTH), BF16)
        vpad_ref[0:ATT_PAD, :] = jnp.zeros((ATT_PAD, ATT_WIDTH), BF16)
        kpad_ref[ATT_PAD:, :] = k_ref[0]
        vpad_ref[ATT_PAD:, :] = v_ref[0]

    start = pl.multiple_of(i * ATT_TQ, ATT_TQ)
    lo_q = lax.broadcasted_iota(jnp.int32, (ATT_TQ, LANES), 1) < ATT_HEAD_DIM
    lo_k = lax.broadcasted_iota(jnp.int32, (ATT_BAND, LANES), 1) < ATT_HEAD_DIM
    zero = jnp.zeros((ATT_BAND, LANES), BF16)
    one_lo = jnp.where(lo_k, 1.0, 0.0).astype(BF16)
    one_hi = jnp.where(lo_k, 0.0, 1.0).astype(BF16)

    def probs(h, padmask):
        ls = slice((h // 2) * LANES, (h // 2 + 1) * LANES)
        own_q = lo_q if h % 2 == 0 else jnp.logical_not(lo_q)
        q2 = q_ref[0, :, ls]
        qm = jnp.where(own_q, q2, jnp.zeros_like(q2))
        k2 = kpad_ref[pl.ds(start, ATT_BAND), ls]
        s = lax.dot_general(qm, k2, (((1,), (1,)), ((), ())), preferred_element_type=F32)
        s = s + bias_ref[h]
        if padmask is not None:
            s = s + padmask
        m = jnp.max(s, axis=-1, keepdims=True)
        return jnp.exp(s - m).astype(BF16)

    def block(masked):
        padmask = None
        if masked:
            col = lax.broadcasted_iota(jnp.int32, (1, ATT_BAND), 1)
            padmask = jnp.where(col >= (ATT_PAD - start), 0.0, MASK_VALUE).astype(F32)
        for hp in range(ATT_HEADS // 2):
            ls = slice(hp * LANES, (hp + 1) * LANES)
            p = jnp.concatenate([probs(2 * hp, padmask), probs(2 * hp + 1, padmask)], axis=1)
            v2 = vpad_ref[pl.ds(start, ATT_BAND), ls]
            vcat = jnp.concatenate(
                [jnp.concatenate([jnp.where(lo_k, v2, zero), one_lo], axis=1),
                 jnp.concatenate([jnp.where(lo_k, zero, v2), one_hi], axis=1)], axis=0)
            out = jnp.dot(p, vcat, preferred_element_type=F32)
            o_ref[0, :, ls] = (out[:, :LANES] / out[:, LANES:]).astype(BF16)

    n_padded = ATT_PAD // ATT_TQ
    pl.when(i < n_padded)(lambda: block(True))
    pl.when(i >= n_padded)(lambda: block(False))


def _attn(proj3, line):
    b, s, _ = proj3.shape
    return pl.pallas_call(
        _attn_kernel,
        grid=(b, s // ATT_TQ),
        in_specs=[pl.BlockSpec((1, ATT_TQ, ATT_WIDTH), lambda bi, i: (bi, i, 1)),
                  pl.BlockSpec((1, s, ATT_WIDTH), lambda bi, i: (bi, 0, 2)),
                  pl.BlockSpec((1, s, ATT_WIDTH), lambda bi, i: (bi, 0, 3)),
                  _const_spec(line.shape)],
        out_specs=pl.BlockSpec((1, ATT_TQ, ATT_WIDTH), lambda bi, i: (bi, i, 0)),
        out_shape=jax.ShapeDtypeStruct((b, s, ATT_WIDTH), BF16),
        scratch_shapes=[pltpu.VMEM((s + ATT_PAD, ATT_WIDTH), BF16),
                        pltpu.VMEM((s + ATT_PAD, ATT_WIDTH), BF16),
                        pltpu.VMEM((ATT_HEADS, ATT_TQ, ATT_BAND), F32)],
        compiler_params=_params(2),
        name="attn",
    )(proj3, proj3, proj3, line)


def _mix_kernel(x_ref, ys_ref, at_ref, ga_ref, gb_ref, g_ref,
                wv_ref, wgt_ref, wos_ref, woa_ref, wo_ref, o_ref):
    ya = jax.nn.gelu(ys_ref[...].astype(F32), approximate=True).astype(BF16)
    val = jnp.dot(ya, wv_ref[...], preferred_element_type=F32)
    gate = jnp.dot(ya, wgt_ref[...], preferred_element_type=F32)
    z = (val * jax.nn.sigmoid(gate)).astype(BF16)
    y_a = jnp.dot(z, wos_ref[...], preferred_element_type=F32)
    y_b = jnp.dot(at_ref[...], woa_ref[...], preferred_element_type=F32)
    merged = (jax.nn.sigmoid(ga_ref[...].astype(F32)) * y_a
              + jax.nn.sigmoid(gb_ref[...].astype(F32)) * y_b)
    m = jnp.dot(merged.astype(BF16), wo_ref[...], preferred_element_type=F32)
    o_ref[...] = x_ref[...] + _rmsnorm(m, g_ref[...])


def _mix(x, ys, at, proj, gains, l, wv, wgt, wos, woa, wo):
    m = x.shape[0]
    row = lambda w: pl.BlockSpec((ROW_TILE, w), lambda i: (i, 0))
    return pl.pallas_call(
        _mix_kernel,
        grid=(m // ROW_TILE,),
        in_specs=[row(D_MODEL), row(SSM_WIDTH), row(ATT_WIDTH),
                  pl.BlockSpec((ROW_TILE, D_MODEL), lambda i: (i, 2)),
                  pl.BlockSpec((ROW_TILE, D_MODEL), lambda i: (i, 3)),
                  _layer_spec((1, D_MODEL), l, 3),
                  _layer_spec((SSM_WIDTH, SSM_WIDTH), l), _layer_spec((SSM_WIDTH, SSM_WIDTH), l),
                  _layer_spec((SSM_WIDTH, D_MODEL), l), _layer_spec((ATT_WIDTH, D_MODEL), l),
                  _layer_spec((D_MODEL, D_MODEL), l)],
        out_specs=row(D_MODEL),
        out_shape=jax.ShapeDtypeStruct((m, D_MODEL), F32),
        compiler_params=_params(1),
        name="mix_out",
    )(x, ys, at, proj, proj, gains, wv, wgt, wos, woa, wo)


def kernel(x, norm_gains, ffn_w_gate, ffn_w_up, ffn_w_down, w_in, lam_re, lam_im, log_dt,
           b_re, b_im, c_re, c_im, d_skip, w_glu_val, w_glu_gate, w_out_ssm, rel_bias,
           w_out_att, w_o):
    bsz, seq, _ = x.shape
    m = bsz * seq
    depth = norm_gains.shape[0]
    gains = norm_gains.astype(F32).reshape(depth, 6, 1, D_MODEL)
    wg, wu, wd = ffn_w_gate.astype(BF16), ffn_w_up.astype(BF16), ffn_w_down.astype(BF16)
    w_in, w_o = w_in.astype(BF16), w_o.astype(BF16)
    wv, wgt = w_glu_val.astype(BF16), w_glu_gate.astype(BF16)
    wos, woa = w_out_ssm.astype(BF16), w_out_att.astype(BF16)
    xf = x.reshape(m, D_MODEL)
    for l in range(depth):
        xf = _ffn(xf, gains, l, 0, wg, wu, wd)
        proj = _proj(xf, gains, l, w_in)
        proj3 = proj.reshape(bsz, seq, IN_WIDTH)
        tabs = _s5_tables(lam_re[l], lam_im[l], log_dt[l], b_re[l], b_im[l], c_re[l], c_im[l],
                          d_skip[l])
        ys = _s5(proj3, *tabs).reshape(m, SSM_WIDTH)
        at = _attn(proj3, _attn_bias_line(rel_bias[l])).reshape(m, ATT_WIDTH)
        xf = _mix(xf, ys, at, proj, gains, l, wv, wgt, wos, woa, w_o)
        xf = _ffn(xf, gains, l, 1, wg, wu, wd)
    return xf.reshape(bsz, seq, D_MODEL)
```

```python
import functools

import jax
import jax.numpy as jnp
import numpy as np
from jax import lax
from jax.experimental import pallas as pl
from jax.experimental.pallas import tpu as pltpu

F32 = jnp.float32
BF16 = jnp.bfloat16

D_MODEL = 1024
D_FF = 2816
CHUNK = 64
SSM_WIDTH = 512
SSM_GROUP = 16
SSM_GROUPS = 32
SSM_STATE = 64
ATT_WIDTH = 512
ATT_HEAD_DIM = 64
ATT_HEADS = 8
LEFT_CHUNKS = 8
MAX_REL = 128
IN_WIDTH = 4096
RMS_EPS = 1e-6
MASK_VALUE = -1e30

LANES = 128
SUBLANES = 8

ROW_TILE = 1024
ROW_SUB = 256
MIX_SUB = 512
FF_CHUNK = 256
FFN_ROWS = 1024
FFN_SUB = 256
PROJ_CHUNK = 512
VMEM_LIMIT = 56 * 1024 * 1024

S5_SEQS = SUBLANES
S5_TS = 64
S5_PITCH = S5_TS + SUBLANES
S5_NBUF = 3
S5_GPT = LANES // SSM_GROUP
S5_TILES = SSM_WIDTH // LANES
S5_NST = SSM_GROUPS * SSM_STATE
S5_TST = S5_GPT * SSM_STATE

ATT_TQ = 256
ATT_SPB = 2
ATT_PAD = LEFT_CHUNKS * CHUNK
ATT_BAND = ATT_TQ + ATT_PAD
ATT_TAB = 1024


def _const_spec(shape):
    nd = len(shape)
    return pl.BlockSpec(shape, lambda *_: (0,) * nd, pipeline_mode=pl.Buffered(1))


def _layer_spec(shape, *lead):
    nd = len(shape)
    return pl.BlockSpec((None,) * len(lead) + shape, lambda *_: lead + (0,) * nd,
                        pipeline_mode=pl.Buffered(1))


def _params(n_axes):
    return pltpu.CompilerParams(dimension_semantics=("arbitrary",) * n_axes,
                                vmem_limit_bytes=VMEM_LIMIT)


def _rmsnorm(x, g):
    ms = jnp.mean(x * x, axis=-1, keepdims=True)
    return x * lax.rsqrt(ms + RMS_EPS) * g


def _ffn_kernel(x_ref, gpre_ref, gpost_ref, wg_ref, wu_ref, wd_ref, o_ref, a_ref):
    for r in range(FFN_ROWS // FFN_SUB):
        rows = slice(r * FFN_SUB, (r + 1) * FFN_SUB)
        x = x_ref[rows, :]
        h = _rmsnorm(x, gpre_ref[...]).astype(BF16)
        for c in range(D_FF // FF_CHUNK):
            sl = slice(c * FF_CHUNK, (c + 1) * FF_CHUNK)
            g = jnp.dot(h, wg_ref[:, sl], preferred_element_type=F32)
            u = jnp.dot(h, wu_ref[:, sl], preferred_element_type=F32)
            a_ref[rows, sl] = (g * jax.nn.sigmoid(g) * u).astype(BF16)
        f = jnp.dot(a_ref[rows, :], wd_ref[...], preferred_element_type=F32)
        o_ref[rows, :] = x + 0.5 * _rmsnorm(f, gpost_ref[...])


def _ffn(x, gains, l, j, wg, wu, wd):
    m = x.shape[0]
    row = pl.BlockSpec((FFN_ROWS, D_MODEL), lambda i: (i, 0))
    return pl.pallas_call(
        _ffn_kernel,
        grid=(m // FFN_ROWS,),
        in_specs=[row, _layer_spec((1, D_MODEL), l, 4 * j), _layer_spec((1, D_MODEL), l, 4 * j + 1),
                  _layer_spec((D_MODEL, D_FF), l, j), _layer_spec((D_MODEL, D_FF), l, j),
                  _layer_spec((D_FF, D_MODEL), l, j)],
        out_specs=row,
        out_shape=jax.ShapeDtypeStruct((m, D_MODEL), F32),
        scratch_shapes=[pltpu.VMEM((FFN_ROWS, D_FF), BF16)],
        compiler_params=_params(1),
        name="ffn",
    )(x, gains, gains, wg, wu, wd)


def _proj_kernel(x_ref, g_ref, w_ref, o_ref):
    for r in range(ROW_TILE // ROW_SUB):
        rows = slice(r * ROW_SUB, (r + 1) * ROW_SUB)
        h = _rmsnorm(x_ref[rows, :], g_ref[...]).astype(BF16)
        for c in range(IN_WIDTH // PROJ_CHUNK):
            sl = slice(c * PROJ_CHUNK, (c + 1) * PROJ_CHUNK)
            p = jnp.dot(h, w_ref[:, sl], preferred_element_type=F32)
            if c == 1:
                p = p * (ATT_HEAD_DIM ** -0.5)
            o_ref[rows, sl] = p.astype(BF16)


def _proj(x, gains, l, w_in):
    m = x.shape[0]
    return pl.pallas_call(
        _proj_kernel,
        grid=(m // ROW_TILE,),
        in_specs=[pl.BlockSpec((ROW_TILE, D_MODEL), lambda i: (i, 0)),
                  _layer_spec((1, D_MODEL), l, 2), _layer_spec((D_MODEL, IN_WIDTH), l)],
        out_specs=pl.BlockSpec((ROW_TILE, IN_WIDTH), lambda i: (i, 0)),
        out_shape=jax.ShapeDtypeStruct((m, IN_WIDTH), BF16),
        compiler_params=_params(1),
        name="in_proj",
    )(x, gains, w_in)


def _s5_tables(lam_re, lam_im, log_dt, b_re, b_im, c_re, c_im, d_skip):
    lr, li = lam_re.astype(F32), lam_im.astype(F32)
    dt = jnp.exp(log_dt.astype(F32))[:, None]
    mag = jnp.exp(lr * dt)
    ang = li * dt
    ab_re, ab_im = mag * jnp.cos(ang), mag * jnp.sin(ang)
    nr, ni = ab_re - 1.0, ab_im
    den = lr * lr + li * li
    f_re = (nr * lr + ni * li) / den
    f_im = (ni * lr - nr * li) / den
    br, bi = b_re.astype(F32), b_im.astype(F32)
    bb_re = f_re[..., None] * br - f_im[..., None] * bi
    bb_im = f_re[..., None] * bi + f_im[..., None] * br
    eye = jnp.eye(S5_GPT, dtype=F32)

    def expand_b(bb):
        t = bb.reshape(S5_TILES, S5_GPT, SSM_STATE, SSM_GROUP)
        t = t.transpose(0, 1, 3, 2)[:, :, :, None, :] * eye[None, :, None, :, None]
        return t.reshape(S5_TILES, LANES, S5_TST)

    def expand_c(cc):
        t = cc.reshape(S5_TILES, S5_GPT, SSM_GROUP, SSM_STATE)
        t = t.transpose(0, 1, 3, 2)[:, :, :, None, :] * eye[None, :, None, :, None]
        return t.reshape(S5_TILES, S5_TST, LANES)

    bq = jnp.concatenate([expand_b(bb_re), expand_b(bb_im)], axis=2).astype(BF16)
    cq = jnp.concatenate([expand_c(c_re.astype(F32)), expand_c(-c_im.astype(F32))], axis=1).astype(BF16)
    n_lt = S5_NST // LANES
    rep = lambda a: jnp.broadcast_to(a.reshape(n_lt, 1, LANES), (n_lt, S5_SEQS, LANES))
    return (bq, cq, rep(ab_re), rep(ab_im),
            d_skip.astype(F32).reshape(1, SSM_WIDTH))


def _s5_kernel(u_ref, bq_ref, cq_ref, are_ref, aim_ref, d_ref, y_ref, *scratch):
    sre = scratch[0:S5_NBUF]
    sim = scratch[S5_NBUF:2 * S5_NBUF]
    utm = scratch[2 * S5_NBUF:3 * S5_NBUF]
    cre_ref, cim_ref, ubt_ref, ytm_ref = scratch[3 * S5_NBUF:]
    i = pl.program_id(1)
    n_lt = S5_NST // LANES
    lt_per_q = S5_TST // LANES

    @pl.when(i == 0)
    def _():
        for ref in scratch[:3 * S5_NBUF] + (cre_ref, cim_ref):
            ref[...] = jnp.zeros_like(ref)

    def stages(slot_in):
        slot_scan = (slot_in + S5_NBUF - 1) % S5_NBUF
        slot_out = (slot_in + S5_NBUF - 2) % S5_NBUF

        for b in range(S5_SEQS):
            ub = u_ref[b].astype(F32)
            for q in range(S5_TILES):
                ubt_ref[q, b * S5_PITCH:b * S5_PITCH + S5_TS, :] = ub[:, q * LANES:(q + 1) * LANES]
        for t in range(S5_TS):
            for q in range(S5_TILES):
                utm[slot_in][q, t * S5_SEQS:(t + 1) * S5_SEQS, :] = (
                    ubt_ref[q, pl.ds(t, S5_SEQS, stride=S5_PITCH), :])
        u_in = jnp.concatenate([utm[slot_in][q] for q in range(S5_TILES)], axis=1).astype(BF16)
        for q in range(S5_TILES):
            bu = jnp.dot(u_in[:, q * LANES:(q + 1) * LANES], bq_ref[q], preferred_element_type=F32)
            for k in range(lt_per_q):
                sre[slot_in][q * lt_per_q + k] = bu[:, k * LANES:(k + 1) * LANES]
                sim[slot_in][q * lt_per_q + k] = bu[:, S5_TST + k * LANES:S5_TST + (k + 1) * LANES]

        for j in range(n_lt):
            a_re = are_ref[j]
            a_im = aim_ref[j]
            s_re = cre_ref[j]
            s_im = cim_ref[j]
            for t in range(S5_TS):
                idx = slice(t * S5_SEQS, (t + 1) * S5_SEQS)
                n_re = a_re * s_re - a_im * s_im + sre[slot_scan][j, idx, :]
                n_im = a_re * s_im + a_im * s_re + sim[slot_scan][j, idx, :]
                sre[slot_scan][j, idx, :] = n_re
                sim[slot_scan][j, idx, :] = n_im
                s_re, s_im = n_re, n_im
            cre_ref[j] = s_re
            cim_ref[j] = s_im

        u_out = jnp.concatenate([utm[slot_out][q] for q in range(S5_TILES)], axis=1)
        du = d_ref[...] * u_out
        for q in range(S5_TILES):
            tiles = ([sre[slot_out][q * lt_per_q + k] for k in range(lt_per_q)]
                     + [sim[slot_out][q * lt_per_q + k] for k in range(lt_per_q)])
            s = jnp.concatenate(tiles, axis=1).astype(BF16)
            ytm_ref[q] = jnp.dot(s, cq_ref[q], preferred_element_type=F32) + du[:, q * LANES:(q + 1) * LANES]
        for b in range(S5_SEQS):
            for q in range(S5_TILES):
                yb = ytm_ref[q, pl.ds(b, S5_TS, stride=S5_SEQS), :]
                y_ref[b, :, q * LANES:(q + 1) * LANES] = yb.astype(BF16)

    for r in range(S5_NBUF):
        pl.when(lax.rem(i, S5_NBUF) == r)(functools.partial(stages, r))


def _s5(proj3, bq, cq, a_re, a_im, d):
    b, s, _ = proj3.shape
    rows = S5_SEQS * S5_TS
    n_lt = S5_NST // LANES
    n_t = s // S5_TS
    return pl.pallas_call(
        _s5_kernel,
        grid=(b // S5_SEQS, n_t + S5_NBUF - 1),
        in_specs=[pl.BlockSpec((S5_SEQS, S5_TS, SSM_WIDTH), lambda g, i: (g, jnp.minimum(i, n_t - 1), 0)),
                  _const_spec(bq.shape), _const_spec(cq.shape), _const_spec(a_re.shape),
                  _const_spec(a_im.shape), _const_spec(d.shape)],
        out_specs=pl.BlockSpec((S5_SEQS, S5_TS, SSM_WIDTH),
                               lambda g, i: (g, jnp.maximum(i - (S5_NBUF - 1), 0), 0)),
        out_shape=jax.ShapeDtypeStruct((b, s, SSM_WIDTH), BF16),
        scratch_shapes=([pltpu.VMEM((n_lt, rows, LANES), F32)] * (2 * S5_NBUF)
                        + [pltpu.VMEM((S5_TILES, rows, LANES), F32)] * S5_NBUF
                        + [pltpu.VMEM((n_lt, S5_SEQS, LANES), F32), pltpu.VMEM((n_lt, S5_SEQS, LANES), F32),
                           pltpu.VMEM((S5_TILES, S5_SEQS * S5_PITCH, LANES), F32),
                           pltpu.VMEM((S5_TILES, rows, LANES), F32)]),
        compiler_params=_params(2),
        name="s5",
    )(proj3, bq, cq, a_re, a_im, d)


def _attn_bias_line(rel_bias):
    rb = rel_bias.astype(F32)
    n_far = ATT_BAND - 1 - MAX_REL
    n_neg = ATT_TAB - n_far - (2 * MAX_REL + 1)
    return jnp.concatenate([jnp.broadcast_to(rb[:, -1:], (ATT_HEADS, n_far)), rb[:, ::-1],
                            jnp.broadcast_to(rb[:, :1], (ATT_HEADS, n_neg))], axis=1)


def _attn_kernel(q_ref, k_ref, v_ref, line_ref, o_ref, kpad_ref, vpad_ref, bias_ref):
    i = pl.program_id(1)

    @pl.when(jnp.logical_and(pl.program_id(0) == 0, i == 0))
    def _():
        qi = lax.broadcasted_iota(jnp.int32, (ATT_TQ, ATT_BAND), 0)
        kj = lax.broadcasted_iota(jnp.int32, (ATT_TQ, ATT_BAND), 1)
        dc = kj // CHUNK - qi // CHUNK
        in_band = jnp.logical_and(dc >= 0, dc <= LEFT_CHUNKS)
        for h in range(ATT_HEADS):
            line = jnp.broadcast_to(line_ref[h:h + 1, :], (ATT_TQ, ATT_TAB))
            skew = pltpu.roll(line, ATT_TAB - (ATT_TQ - 1), 1, stride=1, stride_axis=0)
            bias_ref[h] = jnp.where(in_band, skew[:, :ATT_BAND], MASK_VALUE)

    @pl.when(i == 0)
    def _():
        kpad_ref[0:ATT_PAD, :] = jnp.zeros((ATT_PAD, ATT_WIDTH), BF16)
        vpad_ref[0:ATT_PAD, :] = jnp.zeros((ATT_PAD, ATT_WIDTH), BF16)
        kpad_ref[ATT_PAD:, :] = k_ref[0]
        vpad_ref[ATT_PAD:, :] = v_ref[0]

    lo_q = lax.broadcasted_iota(jnp.int32, (ATT_TQ, LANES), 1) < ATT_HEAD_DIM
    lo_k = lax.broadcasted_iota(jnp.int32, (ATT_BAND, LANES), 1) < ATT_HEAD_DIM
    zero = jnp.zeros((ATT_BAND, LANES), BF16)
    one_lo = jnp.where(lo_k, 1.0, 0.0).astype(BF16)
    one_hi = jnp.where(lo_k, 0.0, 1.0).astype(BF16)

    def probs(h, sb, start, padmask):
        ls = slice((h // 2) * LANES, (h // 2 + 1) * LANES)
        own_q = lo_q if h % 2 == 0 else jnp.logical_not(lo_q)
        q2 = q_ref[0, sb * ATT_TQ:(sb + 1) * ATT_TQ, ls]
        qm = jnp.where(own_q, q2, jnp.zeros_like(q2))
        k2 = kpad_ref[pl.ds(start, ATT_BAND), ls]
        s = lax.dot_general(qm, k2, (((1,), (1,)), ((), ())), preferred_element_type=F32)
        s = s + bias_ref[h]
        if padmask is not None:
            s = s + padmask
        m = jnp.max(s, axis=-1, keepdims=True)
        return jnp.exp(s - m).astype(BF16)

    def blocks(masked):
        for hp in range(ATT_HEADS // 2):
            ls = slice(hp * LANES, (hp + 1) * LANES)
            for sb in range(ATT_SPB):
                start = pl.multiple_of((i * ATT_SPB + sb) * ATT_TQ, ATT_TQ)
                padmask = None
                if masked:
                    col = lax.broadcasted_iota(jnp.int32, (1, ATT_BAND), 1)
                    padmask = jnp.where(col >= (ATT_PAD - start), 0.0, MASK_VALUE).astype(F32)
                p = jnp.concatenate([probs(2 * hp, sb, start, padmask),
                                     probs(2 * hp + 1, sb, start, padmask)], axis=1)
                v2 = vpad_ref[pl.ds(start, ATT_BAND), ls]
                vcat = jnp.concatenate(
                    [jnp.concatenate([jnp.where(lo_k, v2, zero), one_lo], axis=1),
                     jnp.concatenate([jnp.where(lo_k, zero, v2), one_hi], axis=1)], axis=0)
                out = jnp.dot(p, vcat, preferred_element_type=F32)
                o_ref[0, sb * ATT_TQ:(sb + 1) * ATT_TQ, ls] = (out[:, :LANES] / out[:, LANES:]).astype(BF16)

    n_padded = ATT_PAD // (ATT_TQ * ATT_SPB)
    pl.when(i < n_padded)(lambda: blocks(True))
    pl.when(i >= n_padded)(lambda: blocks(False))


def _attn(proj3, line):
    b, s, _ = proj3.shape
    return pl.pallas_call(
        _attn_kernel,
        grid=(b, s // (ATT_TQ * ATT_SPB)),
        in_specs=[pl.BlockSpec((1, ATT_TQ * ATT_SPB, ATT_WIDTH), lambda bi, i: (bi, i, 1)),
                  pl.BlockSpec((1, s, ATT_WIDTH), lambda bi, i: (bi, 0, 2)),
                  pl.BlockSpec((1, s, ATT_WIDTH), lambda bi, i: (bi, 0, 3)),
                  _const_spec(line.shape)],
        out_specs=pl.BlockSpec((1, ATT_TQ * ATT_SPB, ATT_WIDTH), lambda bi, i: (bi, i, 0)),
        out_shape=jax.ShapeDtypeStruct((b, s, ATT_WIDTH), BF16),
        scratch_shapes=[pltpu.VMEM((s + ATT_PAD, ATT_WIDTH), BF16),
                        pltpu.VMEM((s + ATT_PAD, ATT_WIDTH), BF16),
                        pltpu.VMEM((ATT_HEADS, ATT_TQ, ATT_BAND), F32)],
        compiler_params=_params(2),
        name="attn",
    )(proj3, proj3, proj3, line)


def _mix_kernel(x_ref, ys_ref, at_ref, ga_ref, gb_ref, g_ref,
                wv_ref, wgt_ref, wos_ref, woa_ref, wo_ref, o_ref):
    for r in range(ROW_TILE // MIX_SUB):
        rows = slice(r * MIX_SUB, (r + 1) * MIX_SUB)
        ya = jax.nn.gelu(ys_ref[rows, :].astype(F32), approximate=True).astype(BF16)
        val = jnp.dot(ya, wv_ref[...], preferred_element_type=F32)
        gate = jnp.dot(ya, wgt_ref[...], preferred_element_type=F32)
        z = (val * jax.nn.sigmoid(gate)).astype(BF16)
        y_a = jnp.dot(z, wos_ref[...], preferred_element_type=F32)
        y_b = jnp.dot(at_ref[rows, :], woa_ref[...], preferred_element_type=F32)
        merged = (jax.nn.sigmoid(ga_ref[rows, :].astype(F32)) * y_a
                  + jax.nn.sigmoid(gb_ref[rows, :].astype(F32)) * y_b)
        m = jnp.dot(merged.astype(BF16), wo_ref[...], preferred_element_type=F32)
        o_ref[rows, :] = x_ref[rows, :] + _rmsnorm(m, g_ref[...])


def _mix(x, ys, at, proj, gains, l, wv, wgt, wos, woa, wo):
    m = x.shape[0]
    row = lambda w: pl.BlockSpec((ROW_TILE, w), lambda i: (i, 0))
    return pl.pallas_call(
        _mix_kernel,
        grid=(m // ROW_TILE,),
        in_specs=[row(D_MODEL), row(SSM_WIDTH), row(ATT_WIDTH),
                  pl.BlockSpec((ROW_TILE, D_MODEL), lambda i: (i, 2)),
                  pl.BlockSpec((ROW_TILE, D_MODEL), lambda i: (i, 3)),
                  _layer_spec((1, D_MODEL), l, 3),
                  _layer_spec((SSM_WIDTH, SSM_WIDTH), l), _layer_spec((SSM_WIDTH, SSM_WIDTH), l),
                  _layer_spec((SSM_WIDTH, D_MODEL), l), _layer_spec((ATT_WIDTH, D_MODEL), l),
                  _layer_spec((D_MODEL, D_MODEL), l)],
        out_specs=row(D_MODEL),
        out_shape=jax.ShapeDtypeStruct((m, D_MODEL), F32),
        compiler_params=_params(1),
        name="mix_out",
    )(x, ys, at, proj, proj, gains, wv, wgt, wos, woa, wo)


def kernel(x, norm_gains, ffn_w_gate, ffn_w_up, ffn_w_down, w_in, lam_re, lam_im, log_dt,
           b_re, b_im, c_re, c_im, d_skip, w_glu_val, w_glu_gate, w_out_ssm, rel_bias,
           w_out_att, w_o):
    bsz, seq, _ = x.shape
    m = bsz * seq
    depth = norm_gains.shape[0]
    gains = norm_gains.astype(F32).reshape(depth, 6, 1, D_MODEL)
    wg, wu, wd = ffn_w_gate.astype(BF16), ffn_w_up.astype(BF16), ffn_w_down.astype(BF16)
    w_in, w_o = w_in.astype(BF16), w_o.astype(BF16)
    wv, wgt = w_glu_val.astype(BF16), w_glu_gate.astype(BF16)
    wos, woa = w_out_ssm.astype(BF16), w_out_att.astype(BF16)
    xf = x.reshape(m, D_MODEL)
    for l in range(depth):
        xf = _ffn(xf, gains, l, 0, wg, wu, wd)
        proj = _proj(xf, gains, l, w_in)
        proj3 = proj.reshape(bsz, seq, IN_WIDTH)
        tabs = _s5_tables(lam_re[l], lam_im[l], log_dt[l], b_re[l], b_im[l], c_re[l], c_im[l],
                          d_skip[l])
        ys = _s5(proj3, *tabs).reshape(m, SSM_WIDTH)
        at = _attn(proj3, _attn_bias_line(rel_bias[l])).reshape(m, ATT_WIDTH)
        xf = _mix(xf, ys, at, proj, gains, l, wv, wgt, wos, woa, w_o)
        xf = _ffn(xf, gains, l, 1, wg, wu, wd)
    return xf.reshape(bsz, seq, D_MODEL)
```
